```python
import math
import jax, jax.numpy as jnp
from jax import lax
import numpy as np

D_MODEL = 1024
BATCH = 8
SEQ = 4096
DEPTH = 2
DEC_BATCH = 32
DEC_SEQ = 1
PAST_LEN = 16384
PAGE_SIZE = 128

F32 = jnp.float32
N_HEADS_GROUP = 4
HEAD_DIM = D_MODEL // 16
GROUP_W = N_HEADS_GROUP * HEAD_DIM
N_GROUPS = 4
MIX_W = N_GROUPS * GROUP_W
DIFF_QK = HEAD_DIM // 2
ROPE_THETA = 10000.0
Q_BLOCK = 128
SPARSE_Q_BLOCK = 32
MOBA_BLOCK = 256
MOBA_TOPK = 3
NSA_KV = HEAD_DIM
NSA_CMP_LEN = 32
NSA_CMP_STRIDE = 16
NSA_CMP_HIDDEN = 2 * HEAD_DIM
NSA_SLC_BLOCK = 64
NSA_SLC_N = 16
NSA_WINDOW = 512
PEER_HEADS = 8
PEER_NKEYS = 128
PEER_N = PEER_NKEYS * PEER_NKEYS
PEER_KEY_DIM = 128
PEER_TOPK = 16
PEER_TOK_BLOCK = 256
ALPHA = (2.0 * DEPTH) ** 0.25
BETA = (8.0 * DEPTH) ** -0.25
LN_EPS = 1e-5
FOX_ROW = 2 * GROUP_W + N_HEADS_GROUP
DIFF_ROW = 2 * GROUP_W
MOBA_ROW = 2 * GROUP_W
NSA_ROW = 4 * NSA_KV
WIN_ROW = 2 * NSA_KV
IN_WIDTHS = (GROUP_W, GROUP_W, GROUP_W, N_HEADS_GROUP,
             GROUP_W, GROUP_W, GROUP_W,
             GROUP_W, GROUP_W, GROUP_W,
             GROUP_W, NSA_KV, NSA_KV, NSA_KV, NSA_KV, NSA_KV, NSA_KV,
             3 * N_HEADS_GROUP)
D_IN = 10 * GROUP_W + N_HEADS_GROUP + 6 * NSA_KV + 3 * N_HEADS_GROUP

kernel_name = 'hymba_fox_diff_moba_nsa_peer_step'


def _rope(x, pos):
    half = x.shape[-1] // 2
    inv = ROPE_THETA ** (-jnp.arange(half, dtype=F32) / half)
    ang = pos.astype(F32)[:, None] * inv[None, :]
    cos = jnp.cos(ang)[:, None, :]
    sin = jnp.sin(ang)[:, None, :]
    xf = x.astype(F32)
    x1, x2 = xf[..., :half], xf[..., half:]
    return jnp.concatenate([x1 * cos - x2 * sin, x1 * sin + x2 * cos], -1).astype(x.dtype)


def _masked_softmax(s, mask):
    s = jnp.where(mask, s.astype(F32), -jnp.inf)
    m = jnp.max(s, axis=-1, keepdims=True)
    m = jnp.where(jnp.isfinite(m), m, 0.0)
    p = jnp.exp(s - m)
    den = jnp.sum(p, axis=-1, keepdims=True)
    return p / jnp.where(den > 0, den, 1.0)


def _layernorm(x, g, b):
    xf = x.astype(F32)
    mu = jnp.mean(xf, -1, keepdims=True)
    var = jnp.mean(jnp.square(xf - mu), -1, keepdims=True)
    return ((xf - mu) * lax.rsqrt(var + LN_EPS) * g + b).astype(x.dtype)


def _headnorm(x, g):
    xf = x.astype(F32)
    return (xf * lax.rsqrt(jnp.mean(xf * xf, -1, keepdims=True) + LN_EPS) * g).astype(x.dtype)


def _sweep(fn, blk, pos, qs, pre=()):
    T = pos.shape[0]
    nb = T // blk
    qb = tuple(jnp.swapaxes(q.reshape(q.shape[0], nb, blk, *q.shape[2:]), 0, 1) for q in qs)
    out = lax.map(lambda a: fn(*a), (pos.reshape(nb, blk),) + qb + tuple(pre))
    out = jnp.swapaxes(out, 0, 1)
    return out.reshape(out.shape[0], T, *out.shape[3:])


def _fox_attend(pos_q, q, fq, k, v, F):
    s = jnp.einsum('bqhd,bkhd->bhqk', q, k).astype(F32) * HEAD_DIM ** -0.5
    s = s + jnp.swapaxes(fq, 1, 2)[..., :, None] - jnp.swapaxes(F, 1, 2)[..., None, :]
    mask = jnp.arange(k.shape[1])[None, :] <= pos_q[:, None]
    p = _masked_softmax(s, mask)
    return jnp.einsum('bhqk,bkhd->bqhd', p.astype(v.dtype), v)


def _diff_attend(pos_q, q, k, v, lam):
    s = jnp.einsum('bqhcd,bkhcd->bhcqk', q, k).astype(F32) * DIFF_QK ** -0.5
    mask = jnp.arange(k.shape[1])[None, :] <= pos_q[:, None]
    p = _masked_softmax(s, mask)
    a = p[:, :, 0] - lam * p[:, :, 1]
    return jnp.einsum('bhqk,bkhd->bqhd', a.astype(v.dtype), v)


def _moba_attend(pos_q, q, kbt, vbt, kmean):
    B, Tq, H, d = q.shape
    nb = kbt.shape[2]
    own = pos_q // MOBA_BLOCK
    gate = jnp.einsum('bqhd,bnhd->bqhn', q.astype(F32), kmean)
    past_ok = (jnp.arange(nb)[None, :] < own[:, None])[None, :, None, :]
    gate = jnp.where(past_ok, gate, -jnp.inf)
    top_v, top_i = lax.top_k(gate, min(MOBA_TOPK, nb))
    idx = jnp.concatenate([top_i, jnp.broadcast_to(own[None, :, None, None], (B, Tq, H, 1))], -1)
    ok = jnp.concatenate([top_v > -jnp.inf, jnp.ones((B, Tq, H, 1), bool)], -1)
    b_ix = jnp.arange(B)[:, None, None, None]
    h_ix = jnp.arange(H)[None, None, :, None]
    ks = kbt[b_ix, h_ix, idx]
    vs = vbt[b_ix, h_ix, idx]
    kpos = idx[..., None] * MOBA_BLOCK + jnp.arange(MOBA_BLOCK)
    mask = ok[..., None] & (kpos <= pos_q[None, :, None, None, None])
    s = jnp.einsum('bqhd,bqhnkd->bqhnk', q, ks).astype(F32) * d ** -0.5
    p = _masked_softmax(s.reshape(B, Tq, H, -1), mask.reshape(B, Tq, H, -1))
    return jnp.einsum('bqhm,bqhmd->bqhd', p.astype(vs.dtype), vs.reshape(B, Tq, H, -1, d))


def _nsa_compress(rows2, pos_emb, w1, b1, w2, b2):
    _, B, L, kv = rows2.shape
    S = NSA_CMP_STRIDE
    Lp = -(-L // S) * S
    r = jnp.pad(rows2, ((0, 0), (0, 0), (0, Lp - L), (0, 0)))
    ch = r.reshape(2, B, Lp // S, S, kv)
    blocks = jnp.concatenate([ch[:, :, :-1], ch[:, :, 1:]], axis=3) + pos_emb[:, None, None]
    flat = blocks.reshape(2, B, Lp // S - 1, NSA_CMP_LEN * kv)
    h = jax.nn.gelu(jnp.einsum('cbnf,cfh->cbnh', flat, w1) + b1[:, None, None])
    return jnp.einsum('cbnh,chd->cbnd', h, w2) + b2[:, None, None]


def _nsa_attend(pos_q, q, qr, g, band, band_pos, ck, cv, skb, svb):
    B, Tq, H, d = q.shape
    scale = d ** -0.5
    nc = ck.shape[1]
    c_end = jnp.arange(nc) * NSA_CMP_STRIDE + (NSA_CMP_LEN - 1)
    s_c = jnp.einsum('bqhd,bnd->bhqn', q, ck).astype(F32) * scale
    p_c = _masked_softmax(s_c, c_end[None, :] <= pos_q[:, None])
    o_c = jnp.einsum('bhqn,bnd->bqhd', p_c.astype(cv.dtype), cv)
    ns = skb.shape[1]
    r = NSA_SLC_BLOCK // NSA_CMP_STRIDE
    imp = jnp.pad(p_c.sum(1), ((0, 0), (0, 0), (0, ns * r - nc))).reshape(B, Tq, ns, r)
    imp = imp.sum(-1) + jnp.pad(imp[..., :-1, r - 1], ((0, 0), (0, 0), (1, 0)))
    cur = pos_q // NSA_SLC_BLOCK
    j = jnp.arange(ns)[None, :]
    forced = (j == 0) | (j == cur[:, None]) | (j == cur[:, None] - 1)
    score = jnp.where(forced, jnp.inf, imp)
    score = jnp.where(j <= cur[:, None], score, -jnp.inf)
    top_v, top_i = lax.top_k(score, min(NSA_SLC_N, ns))
    b_ix = jnp.arange(B)[:, None, None]
    ks = skb[b_ix, top_i]
    vs = svb[b_ix, top_i]
    kpos = top_i[..., None] * NSA_SLC_BLOCK + jnp.arange(NSA_SLC_BLOCK)
    m_s = (top_v > -jnp.inf)[..., None] & (kpos <= pos_q[None, :, None, None])
    s_s = jnp.einsum('bqhd,bqnkd->bqhnk', qr, ks).astype(F32) * scale
    p_s = _masked_softmax(s_s.reshape(B, Tq, H, -1), m_s.reshape(B, Tq, 1, -1))
    o_s = jnp.einsum('bqhm,bqmd->bqhd', p_s.astype(vs.dtype), vs.reshape(B, Tq, -1, d))
    wk, wv = band[..., :NSA_KV], band[..., NSA_KV:]
    dist = pos_q[:, None] - band_pos[None, :]
    m_w = (dist >= 0) & (dist < NSA_WINDOW) & (band_pos[None, :] >= 0)
    s_w = jnp.einsum('bqhd,bkd->bhqk', qr, wk).astype(F32) * scale
    p_w = _masked_softmax(s_w, m_w)
    o_w = jnp.einsum('bhqk,bkd->bqhd', p_w.astype(wv.dtype), wv)
    return g[..., 0:1] * o_c + g[..., 1:2] * o_s + g[..., 2:3] * o_w


def _peer(x, wq, subkeys, u_tab, v_tab):
    n, dm = x.shape
    blk = min(PEER_TOK_BLOCK, n)
    npad = -(-n // blk) * blk
    xb = jnp.pad(x, ((0, npad - n), (0, 0))).reshape(npad // blk, blk, dm)
    half = PEER_KEY_DIM // 2

    def block(xt):
        q = (xt @ wq).reshape(blk, PEER_HEADS, 2, half)
        s = jnp.einsum('thcd,ckd->thck', q, subkeys).astype(F32)
        v1, i1 = lax.top_k(s[:, :, 0], PEER_TOPK)
        v2, i2 = lax.top_k(s[:, :, 1], PEER_TOPK)
        cand = (v1[..., :, None] + v2[..., None, :]).reshape(blk, PEER_HEADS, -1)
        cidx = (i1[..., :, None] * PEER_NKEYS + i2[..., None, :]).reshape(blk, PEER_HEADS, -1)
        sv, si = lax.top_k(cand, PEER_TOPK)
        eidx = jnp.take_along_axis(cidx, si, axis=-1)
        g = jax.nn.softmax(sv, axis=-1)
        act = jax.nn.gelu(jnp.einsum('thkd,td->thk', u_tab[eidx], xt).astype(F32))
        w = (g * act).astype(xt.dtype)
        return jnp.einsum('thk,thkd->td', w, v_tab[eidx])

    return lax.map(block, xb).reshape(npad, dm)[:n]


def _project(x, pos, w_in, fox_bf):
    B, T, _ = x.shape
    H, d = N_HEADS_GROUP, HEAD_DIM
    split_at = np.cumsum(IN_WIDTHS)[:-1].tolist()
    (fq, fk, fv, ff, dq, dk, dv, mq, mk, mv, nq, nck, ncv, nsk, nsv, nwk, nwv, ng) = jnp.split(x @ w_in, split_at, axis=-1)
    heads = lambda a: a.reshape(B, T, H, d)
    logf = jax.nn.log_sigmoid((ff + fox_bf).astype(F32)).astype(x.dtype)
    dq_r = _rope(dq.reshape(B, T, 2 * H, DIFF_QK), pos).reshape(B, T, H, 2, DIFF_QK)
    dk_r = _rope(dk.reshape(B, T, 2 * H, DIFF_QK), pos).reshape(B, T, GROUP_W)
    mq_r = _rope(heads(mq), pos)
    mk_r = _rope(heads(mk), pos).reshape(B, T, GROUP_W)
    nq_h = heads(nq)
    nq_r = _rope(nq_h, pos)
    nsk_r = _rope(nsk[:, :, None], pos)[:, :, 0]
    nwk_r = _rope(nwk[:, :, None], pos)[:, :, 0]
    gates = jax.nn.sigmoid(ng.reshape(B, T, H, 3))
    queries = (heads(fq), dq_r, mq_r, nq_h, nq_r, gates)
    rows = (jnp.concatenate([fk, fv, logf], -1),
            jnp.concatenate([dk_r, dv], -1),
            jnp.concatenate([mk_r, mv], -1),
            jnp.concatenate([nck, ncv, nsk_r, nsv], -1),
            jnp.concatenate([nwk_r, nwv], -1))
    return queries, rows


def _window_band(win_rows, blk):
    B, T, C = win_rows.shape
    nb = T // blk
    padded = jnp.pad(win_rows, ((0, 0), (NSA_WINDOW, 0), (0, 0)))
    idx = jnp.arange(nb)[:, None] * blk + jnp.arange(NSA_WINDOW + blk)[None, :]
    return jnp.swapaxes(padded[:, idx], 0, 1), (idx - NSA_WINDOW).astype(jnp.int32)


def _gather_pages(cache, l, page_table):
    g = cache[l, page_table]
    return g.reshape(g.shape[0], -1, g.shape[-1])


def _mix(queries, pos, full, band, band_pos, diff_lam, diff_gain, lam_init,
         cmp_pos, cmp_w1, cmp_b1, cmp_w2, cmp_b2, q_blk, s_blk):
    fq, dq, mq, nq, nqr, gates = queries
    fox_rows, diff_rows, moba_rows, nsa_rows = full
    B, L, _ = fox_rows.shape
    H, d, gw, kv = N_HEADS_GROUP, HEAD_DIM, GROUP_W, NSA_KV
    fk = fox_rows[..., :gw].reshape(B, L, H, d)
    fv = fox_rows[..., gw:2 * gw].reshape(B, L, H, d)
    F = jnp.cumsum(fox_rows[..., 2 * gw:].astype(F32), axis=1)
    o_a = _sweep(lambda p, q, fqq: _fox_attend(p, q, fqq, fk, fv, F), q_blk, pos, (fq, jnp.take(F, pos, axis=1)))
    dk = diff_rows[..., :gw].reshape(B, L, H, 2, DIFF_QK)
    dv = diff_rows[..., gw:].reshape(B, L, H, d)
    lam = jnp.exp(jnp.sum(diff_lam[0] * diff_lam[1]).astype(F32)) - jnp.exp(jnp.sum(diff_lam[2] * diff_lam[3]).astype(F32)) + lam_init
    o_b = _sweep(lambda p, q: _diff_attend(p, q, dk, dv, lam), q_blk, pos, (dq,))
    o_b = _headnorm(o_b, diff_gain) * (1.0 - lam_init)
    nbk = -(-L // MOBA_BLOCK)
    mrows = jnp.pad(moba_rows, ((0, 0), (0, nbk * MOBA_BLOCK - L), (0, 0))).reshape(B, nbk, MOBA_BLOCK, 2, H, d)
    kbt = jnp.transpose(mrows[:, :, :, 0], (0, 3, 1, 2, 4))
    vbt = jnp.transpose(mrows[:, :, :, 1], (0, 3, 1, 2, 4))
    kmean = jnp.mean(mrows[:, :, :, 0].astype(F32), axis=2)
    o_c = _sweep(lambda p, q: _moba_attend(p, q, kbt, vbt, kmean), s_blk, pos, (mq,))
    cmp = _nsa_compress(jnp.stack([nsa_rows[..., :kv], nsa_rows[..., kv:2 * kv]]), cmp_pos, cmp_w1, cmp_b1, cmp_w2, cmp_b2)
    ck, cv = cmp[0], cmp[1]
    nsb = -(-L // NSA_SLC_BLOCK)
    srows = jnp.pad(nsa_rows[..., 2 * kv:], ((0, 0), (0, nsb * NSA_SLC_BLOCK - L), (0, 0))).reshape(B, nsb, NSA_SLC_BLOCK, 2, kv)
    skb, svb = srows[:, :, :, 0], srows[:, :, :, 1]
    o_d = _sweep(lambda p, q, qr, g, bd, bp: _nsa_attend(p, q, qr, g, bd, bp, ck, cv, skb, svb),
                 s_blk, pos, (nq, nqr, gates), (band, band_pos))
    o = jnp.concatenate([o_a, o_b, o_c, o_d], axis=2)
    return o.reshape(B, o.shape[1], MIX_W)


def _layer(x, pos, past, past_win, lp, lam_init, q_blk, s_blk):
    (w_in, fox_bf, diff_lam, diff_gain, cmp_pos, cmp_w1, cmp_b1, cmp_w2, cmp_b2,
     w_out, ln1_g, ln1_b, peer_wq, peer_subkeys, peer_u, peer_v, ln2_g, ln2_b) = lp
    B, T, D = x.shape
    queries, rows = _project(x, pos, w_in, fox_bf)
    if past is None:
        full = rows[:4]
        band, band_pos = _window_band(rows[4], s_blk)
        new_win = rows[4][:, -min(NSA_WINDOW, T):]
    else:
        full = tuple(jnp.concatenate([p, r], axis=1) for p, r in zip(past, rows[:4]))
        wb = past_win.shape[1]
        wrows = jnp.concatenate([past_win, rows[4]], axis=1)
        start = past[0].shape[1] - wb
        band = wrows[None]
        band_pos = (start + jnp.arange(wrows.shape[1], dtype=jnp.int32))[None]
        new_win = wrows[:, -wb:]
    mixed = _mix(queries, pos, full, band, band_pos, diff_lam, diff_gain, lam_init,
                 cmp_pos, cmp_w1, cmp_b1, cmp_w2, cmp_b2, q_blk, s_blk)
    h = _layernorm(ALPHA * x + mixed @ w_out, ln1_g, ln1_b)
    f = _peer(h.reshape(B * T, D), peer_wq, peer_subkeys, peer_u, peer_v).reshape(B, T, D)
    y = _layernorm(ALPHA * h + f, ln2_g, ln2_b)
    return y, (rows[0], rows[1], rows[2], rows[3], new_win)


def setup_inputs(seed: int = 0) -> dict:
    key = jax.random.key(seed)
    ks = jax.random.split(key, 32)
    nrm = lambda k, shape, s=1.0: s * jax.random.normal(k, shape, F32)
    H = N_HEADS_GROUP
    n_pages = PAST_LEN // PAGE_SIZE
    used = DEC_BATCH * n_pages
    pool = used + used // 4
    wb = min(NSA_WINDOW, PAST_LEN)
    return {
        'x_prompt': nrm(ks[0], (BATCH, SEQ, D_MODEL)),
        'x_sample': nrm(ks[1], (DEC_BATCH, DEC_SEQ, D_MODEL)),
        'cache_fox': jnp.concatenate([nrm(ks[2], (DEPTH, pool, PAGE_SIZE, 2 * GROUP_W)),
                                      jax.nn.log_sigmoid(2.0 + nrm(ks[3], (DEPTH, pool, PAGE_SIZE, H)))], -1),
        'cache_diff': nrm(ks[4], (DEPTH, pool, PAGE_SIZE, DIFF_ROW)),
        'cache_moba': nrm(ks[5], (DEPTH, pool, PAGE_SIZE, MOBA_ROW)),
        'cache_nsa': nrm(ks[6], (DEPTH, pool, PAGE_SIZE, NSA_ROW)),
        'state_nsa_win': nrm(ks[7], (DEPTH, DEC_BATCH, wb, WIN_ROW)),
        'page_table': jax.random.permutation(ks[8], pool)[:used].reshape(DEC_BATCH, n_pages).astype(jnp.int32),
        'w_in': nrm(ks[9], (DEPTH, D_MODEL, D_IN), D_MODEL ** -0.5),
        'fox_bf': 2.0 + nrm(ks[10], (DEPTH, H), 0.1),
        'diff_lam': nrm(ks[11], (DEPTH, 4, DIFF_QK), 0.1),
        'diff_gain': 1.0 + nrm(ks[12], (DEPTH, HEAD_DIM), 0.02),
        'nsa_cmp_pos': nrm(ks[13], (DEPTH, 2, NSA_CMP_LEN, NSA_KV), 0.02),
        'nsa_cmp_w1': nrm(ks[14], (DEPTH, 2, NSA_CMP_LEN * NSA_KV, NSA_CMP_HIDDEN), (NSA_CMP_LEN * NSA_KV) ** -0.5),
        'nsa_cmp_b1': nrm(ks[15], (DEPTH, 2, NSA_CMP_HIDDEN), 0.01),
        'nsa_cmp_w2': nrm(ks[16], (DEPTH, 2, NSA_CMP_HIDDEN, NSA_KV), NSA_CMP_HIDDEN ** -0.5),
        'nsa_cmp_b2': nrm(ks[17], (DEPTH, 2, NSA_KV), 0.01),
        'w_out': nrm(ks[18], (DEPTH, MIX_W, D_MODEL), BETA * MIX_W ** -0.5),
        'ln1_g': 1.0 + nrm(ks[19], (DEPTH, D_MODEL), 0.02),
        'ln1_b': nrm(ks[20], (DEPTH, D_MODEL), 0.02),
        'peer_wq': nrm(ks[21], (DEPTH, D_MODEL, PEER_HEADS * PEER_KEY_DIM), D_MODEL ** -0.5),
        'peer_subkeys': nrm(ks[22], (DEPTH, 2, PEER_NKEYS, PEER_KEY_DIM // 2), (PEER_KEY_DIM // 2) ** -0.5),
        'peer_u': nrm(ks[23], (DEPTH, PEER_N, D_MODEL), D_MODEL ** -0.5),
        'peer_v': nrm(ks[24], (DEPTH, PEER_N, D_MODEL), BETA * PEER_HEADS ** -0.5),
        'ln2_g': 1.0 + nrm(ks[25], (DEPTH, D_MODEL), 0.02),
        'ln2_b': nrm(ks[26], (DEPTH, D_MODEL), 0.02),
    }


def reference(x_prompt, x_sample, cache_fox, cache_diff, cache_moba, cache_nsa, state_nsa_win, page_table,
              w_in, fox_bf, diff_lam, diff_gain, nsa_cmp_pos, nsa_cmp_w1, nsa_cmp_b1, nsa_cmp_w2, nsa_cmp_b2,
              w_out, ln1_g, ln1_b, peer_wq, peer_subkeys, peer_u, peer_v, ln2_g, ln2_b):
    t_p = x_prompt.shape[1]
    t_s = x_sample.shape[1]
    past_len = page_table.shape[1] * cache_fox.shape[2]
    pos_p = jnp.arange(t_p, dtype=jnp.int32)
    pos_s = past_len + jnp.arange(t_s, dtype=jnp.int32)
    new_p = [[], [], [], [], []]
    new_s = [[], [], [], [], []]
    yp, ys = x_prompt, x_sample
    for l in range(DEPTH):
        lam_init = 0.8 - 0.6 * math.exp(-0.3 * l)
        lp = (w_in[l], fox_bf[l], diff_lam[l], diff_gain[l], nsa_cmp_pos[l], nsa_cmp_w1[l], nsa_cmp_b1[l],
              nsa_cmp_w2[l], nsa_cmp_b2[l], w_out[l], ln1_g[l], ln1_b[l], peer_wq[l], peer_subkeys[l],
              peer_u[l], peer_v[l], ln2_g[l], ln2_b[l])
        yp, rp = _layer(yp, pos_p, None, None, lp, lam_init, min(Q_BLOCK, t_p), min(SPARSE_Q_BLOCK, t_p))
        past = (_gather_pages(cache_fox, l, page_table), _gather_pages(cache_diff, l, page_table),
                _gather_pages(cache_moba, l, page_table), _gather_pages(cache_nsa, l, page_table))
        ys, rs = _layer(ys, pos_s, past, state_nsa_win[l], lp, lam_init, t_s, t_s)
        for i in range(5):
            new_p[i].append(rp[i])
            new_s[i].append(rs[i])
    fox_rows_prompt, fox_rows_sample = jnp.stack(new_p[0]), jnp.stack(new_s[0])
    diff_rows_prompt, diff_rows_sample = jnp.stack(new_p[1]), jnp.stack(new_s[1])
    moba_rows_prompt, moba_rows_sample = jnp.stack(new_p[2]), jnp.stack(new_s[2])
    nsa_rows_prompt, nsa_rows_sample = jnp.stack(new_p[3]), jnp.stack(new_s[3])
    win_prompt, win_sample = jnp.stack(new_p[4]), jnp.stack(new_s[4])
    return (yp, ys, fox_rows_prompt, fox_rows_sample, diff_rows_prompt, diff_rows_sample,
            moba_rows_prompt, moba_rows_sample, nsa_rows_prompt, nsa_rows_sample, win_prompt, win_sample)
```

```python
import functools
import math

import jax
import jax.numpy as jnp
import numpy as np
from jax import lax
from jax.experimental import pallas as pl
from jax.experimental.pallas import tpu as pltpu

F32 = jnp.float32
BF16 = jnp.bfloat16

D_MODEL = 1024
N_HEADS_GROUP = 4
HEAD_DIM = 64
GROUP_W = 256
DIFF_QK = 32
ROPE_THETA = 10000.0
MOBA_BLOCK = 256
MOBA_TOPK = 3
NSA_CMP_LEN = 32
NSA_CMP_STRIDE = 16
NSA_CMP_HIDDEN = 128
NSA_SLC_BLOCK = 64
NSA_SLC_N = 16
NSA_WINDOW = 512
PEER_HEADS = 8
PEER_NKEYS = 128
PEER_TOPK = 16
LN_EPS = 1e-5
PAGE = 128

VMEM_LIMIT = 56 * 1024 * 1024
LANES = 128

NEG = -1e30

_C_FQ, _C_FK, _C_FV = 0, 256, 512
_C_DQ, _C_DK, _C_DV = 768, 1024, 1280
_C_MQ, _C_MK, _C_MV = 1536, 1792, 2048
_C_NQ = 2304
_C_NC, _C_NS, _C_NW, _C_SMALL = 2560, 2688, 2816, 2944
D_IN_PAD = 3072


def _cparams(sem):
    return pltpu.CompilerParams(dimension_semantics=sem, vmem_limit_bytes=VMEM_LIMIT)


def _dot(a, b):
    return lax.dot_general(a, b, (((1,), (0,)), ((), ())), preferred_element_type=F32)


def _dot_nt(a, b):
    return lax.dot_general(a, b, (((1,), (1,)), ((), ())), preferred_element_type=F32)


def _dot_hi(a, b):
    return lax.dot_general(a, b, (((1,), (0,)), ((), ())), preferred_element_type=F32,
                           precision=lax.Precision.HIGHEST)


def _dot_nt_hi(a, b):
    return lax.dot_general(a, b, (((1,), (1,)), ((), ())), preferred_element_type=F32,
                           precision=lax.Precision.HIGHEST)


def _split(a):
    hi = a.astype(BF16)
    lo = (a - hi.astype(F32)).astype(BF16)
    return hi, lo


def _mm3(ahi, alo, bhi, blo):
    return _dot(ahi, bhi) + _dot(ahi, blo) + _dot(alo, bhi)


def _gelu(x):
    c = math.sqrt(2.0 / math.pi)
    return x * (0.5 * (1.0 + jnp.tanh(c * (x + 0.044715 * (x * x * x)))))


def _const_spec(shape):
    nd = len(shape)
    return pl.BlockSpec(shape, lambda *_: (0,) * nd, pipeline_mode=pl.Buffered(1))


def _layernorm(y, g, b):
    mu = jnp.mean(y, axis=-1, keepdims=True)
    d = y - mu
    var = jnp.mean(d * d, axis=-1, keepdims=True)
    return d * lax.rsqrt(var + LN_EPS) * g + b


def _rope128(a, cos, sin, half):
    lane = lax.broadcasted_iota(jnp.int32, a.shape, 1)
    first = (lane % (2 * half)) < half
    partner = jnp.where(first, pltpu.roll(a, LANES - half, 1), pltpu.roll(a, half, 1))
    return a * cos + partner * sin


def _inproj_kernel(x_ref, whi_ref, wlo_ref, c64_ref, s64_ref, c32_ref, s32_ref, bf_ref,
                   fox_ref, diff_ref, moba_ref, nsa_ref, win_ref, lf_ref,
                   fq_ref, dq_ref, mq_ref, nq_ref, nqr_ref, g_ref):
    xhi, xlo = _split(x_ref[...])

    def proj(c0, w):
        return _mm3(xhi, xlo, whi_ref[:, c0:c0 + w], wlo_ref[:, c0:c0 + w])

    c64, s64, c32, s32 = c64_ref[...], s64_ref[...], c32_ref[...], s32_ref[...]

    def rope256(y, cos, sin, half):
        return (_rope128(y[:, :LANES], cos, sin, half), _rope128(y[:, LANES:], cos, sin, half))

    fq_ref[...] = proj(_C_FQ, 256)
    fox_ref[:, 0:256] = proj(_C_FK, 256)
    fox_ref[:, 256:512] = proj(_C_FV, 256)
    a, b = rope256(proj(_C_DQ, 256), c32, s32, DIFF_QK // 2)
    dq_ref[:, 0:128] = a
    dq_ref[:, 128:256] = b
    a, b = rope256(proj(_C_DK, 256), c32, s32, DIFF_QK // 2)
    diff_ref[:, 0:128] = a
    diff_ref[:, 128:256] = b
    diff_ref[:, 256:512] = proj(_C_DV, 256)
    a, b = rope256(proj(_C_MQ, 256), c64, s64, HEAD_DIM // 2)
    mq_ref[:, 0:128] = a
    mq_ref[:, 128:256] = b
    a, b = rope256(proj(_C_MK, 256), c64, s64, HEAD_DIM // 2)
    moba_ref[:, 0:128] = a
    moba_ref[:, 128:256] = b
    moba_ref[:, 256:512] = proj(_C_MV, 256)
    nq = proj(_C_NQ, 256)
    nq_ref[...] = nq
    a, b = rope256(nq, c64, s64, HEAD_DIM // 2)
    nqr_ref[:, 0:128] = a
    nqr_ref[:, 128:256] = b
    rest = proj(_C_NC, 512)
    lane = lax.broadcasted_iota(jnp.int32, (rest.shape[0], LANES), 1)
    nsa_ref[:, 0:128] = rest[:, 0:128]
    t = rest[:, 128:256]
    nsa_ref[:, 128:256] = jnp.where(lane < HEAD_DIM, _rope128(t, c64, s64, HEAD_DIM // 2), t)
    t = rest[:, 256:384]
    win_ref[...] = jnp.where(lane < HEAD_DIM, _rope128(t, c64, s64, HEAD_DIM // 2), t)
    small = rest[:, 384:512]
    z = small + bf_ref[...]
    logf = jnp.minimum(z, 0.0) - jnp.log(1.0 + jnp.exp(-jnp.abs(z)))
    logf = jnp.where(lane < N_HEADS_GROUP, logf, 0.0)
    lf_ref[...] = logf
    fox_ref[:, 512:516] = logf[:, 0:N_HEADS_GROUP]
    g_ref[...] = 1.0 / (1.0 + jnp.exp(-small))


def _rope_tables(pos, half):
    inv = ROPE_THETA ** (-jnp.arange(half, dtype=F32) / half)
    ang = pos.astype(F32)[:, None] * inv[None, :]
    cos, sin = jnp.cos(ang), jnp.sin(ang)
    rep = LANES // (2 * half)
    return (jnp.tile(jnp.concatenate([cos, cos], 1), (1, rep)),
            jnp.tile(jnp.concatenate([-sin, sin], 1), (1, rep)))


def _inproj(x, whi, wlo, bfpad, pos, seq_len):
    n = x.shape[0]
    tb = min(256, seq_len) if seq_len > 1 else n
    nt = max(seq_len // tb, 1)
    if seq_len == 1:
        pos = jnp.broadcast_to(pos, (n,))
    c64, s64 = _rope_tables(pos, HEAD_DIM // 2)
    c32, s32 = _rope_tables(pos, DIFF_QK // 2)
    tok = lambda w: pl.BlockSpec((tb, w), lambda i: (i, 0))
    tab = pl.BlockSpec((tb, LANES), lambda i: (i % nt, 0))
    outs = [(516,), (512,), (512,), (256,), (128,), (128,), (256,), (256,), (256,), (256,), (256,), (128,)]
    return pl.pallas_call(
        _inproj_kernel,
        grid=(n // tb,),
        in_specs=[tok(D_MODEL), _const_spec(whi.shape), _const_spec(wlo.shape), tab, tab, tab, tab,
                  _const_spec((1, LANES))],
        out_specs=[tok(w[0]) for w in outs],
        out_shape=[jax.ShapeDtypeStruct((n, w[0]), F32) for w in outs],
        compiler_params=_cparams(("arbitrary",)),
        name="inproj",
    )(x, whi, wlo, c64, s64, c32, s32, bfpad)


def _online(s, v, m, l, acc):
    m_new = jnp.maximum(m, jnp.max(s, axis=1, keepdims=True))
    a = jnp.exp(m - m_new)
    p = jnp.exp(s - m_new)
    l = a * l + jnp.sum(p, axis=1, keepdims=True)
    acc = a * acc + _dot(p.astype(BF16), v)
    return m_new, l, acc


def _init_state(tq, dv):
    return (jnp.full((tq, 1), NEG, F32), jnp.zeros((tq, 1), F32), jnp.zeros((tq, dv), F32))


def _causal(tq):
    r = lax.broadcasted_iota(jnp.int32, (tq, tq), 0)
    c = lax.broadcasted_iota(jnp.int32, (tq, tq), 1)
    return r >= c


def _seq_specs(nt, tq, widths_q, widths_rows, t):
    qs = [pl.BlockSpec((tq, w), lambda b, i: (b * nt + i, 0)) for w in widths_q]
    rs = [pl.BlockSpec((t, w), lambda b, i: (b, 0)) for w in widths_rows]
    return qs, rs


def _fox_kernel(q_ref, rows_ref, lf_ref, o_ref, kb, vb, fc, fr, *, nt, tq):
    i = pl.program_id(1)

    @pl.when(i == 0)
    def _():
        r = lax.broadcasted_iota(jnp.int32, (tq, tq), 0)
        c = lax.broadcasted_iota(jnp.int32, (tq, tq), 1)
        tri = (r >= c).astype(F32)
        carry = jnp.zeros((1, LANES), F32)
        for t in range(nt):
            sl = slice(t * tq, (t + 1) * tq)
            kb[t] = rows_ref[sl, 0:256].astype(BF16)
            vb[t] = rows_ref[sl, 256:512].astype(BF16)
            f = _dot_hi(tri, lf_ref[sl, :]) + carry
            fc[t] = f
            fr[t] = f.T[0:8, :]
            carry = f[tq - 1:tq, :]

    mask = _causal(tq)
    fci = fc[i]
    for h in range(N_HEADS_GROUP):
        hs = slice(h * HEAD_DIM, (h + 1) * HEAD_DIM)
        q = (q_ref[:, hs] * (HEAD_DIM ** -0.5)).astype(BF16)
        fq = fci[:, h:h + 1]

        def tile(j, st, masked):
            s = _dot_nt(q, kb[j, :, hs]) + fq - fr[j, h:h + 1, :]
            if masked:
                s = jnp.where(mask, s, NEG)
            return _online(s, vb[j, :, hs], *st)

        st = tile(i, _init_state(tq, HEAD_DIM), True)
        m, l, acc = lax.fori_loop(0, i, lambda j, st: tile(j, st, False), st)
        o_ref[:, hs] = acc / l


def _fox_prompt(fq, rows, lf, b, t):
    tq = min(256, t)
    nt = t // tq
    qs, rs = _seq_specs(nt, tq, [256], [516, 128], t)
    return pl.pallas_call(
        functools.partial(_fox_kernel, nt=nt, tq=tq),
        grid=(b, nt),
        in_specs=qs + rs,
        out_specs=pl.BlockSpec((tq, 256), lambda bb, i: (bb * nt + i, 0)),
        out_shape=jax.ShapeDtypeStruct((b * t, 256), F32),
        scratch_shapes=[pltpu.VMEM((nt, tq, 256), BF16), pltpu.VMEM((nt, tq, 256), BF16),
                        pltpu.VMEM((nt, tq, LANES), F32), pltpu.VMEM((nt, 8, tq), F32)],
        compiler_params=_cparams(("arbitrary", "arbitrary")),
        name="fox_prompt",
    )(fq, rows, lf)


def _diff_lambda(lam_ref, lam_init):
    lam = lam_ref[...]
    a = jnp.sum(lam[0:1, :] * lam[1:2, :], axis=1, keepdims=True)
    b = jnp.sum(lam[2:3, :] * lam[3:4, :], axis=1, keepdims=True)
    return jnp.exp(a) - jnp.exp(b) + lam_init


def _diff_finish(o0, o1, lam, gain, lam_init):
    o = o0 - lam * o1
    o = o * lax.rsqrt(jnp.mean(o * o, axis=-1, keepdims=True) + LN_EPS) * gain
    return o * (1.0 - lam_init)


def _diff_kernel(q_ref, rows_ref, lam_ref, gain_ref, o_ref, kb, vb, *, nt, tq, lam_init):
    i = pl.program_id(1)

    @pl.when(i == 0)
    def _():
        for t in range(nt):
            sl = slice(t * tq, (t + 1) * tq)
            kb[t] = rows_ref[sl, 0:256].astype(BF16)
            vb[t] = rows_ref[sl, 256:512].astype(BF16)

    mask = _causal(tq)
    lam = _diff_lambda(lam_ref, lam_init)
    scale = DIFF_QK ** -0.5
    for h in range(N_HEADS_GROUP):
        hs = slice(h * HEAD_DIM, (h + 1) * HEAD_DIM)
        outs = []
        for c in range(2):
            cs = slice(h * HEAD_DIM + c * DIFF_QK, h * HEAD_DIM + (c + 1) * DIFF_QK)
            q = q_ref[:, cs].astype(BF16)

            def tile(j, st, masked):
                s = _dot_nt(q, kb[j, :, cs]) * scale
                if masked:
                    s = jnp.where(mask, s, NEG)
                return _online(s, vb[j, :, hs], *st)

            st = tile(i, _init_state(tq, HEAD_DIM), True)
            m, l, acc = lax.fori_loop(0, i, lambda j, st: tile(j, st, False), st)
            outs.append(acc / l)
        o_ref[:, hs] = _diff_finish(outs[0], outs[1], lam, gain_ref[...], lam_init)


def _diff_prompt(dq, rows, lam4, gain, b, t, lam_init):
    tq = min(256, t)
    nt = t // tq
    qs, rs = _seq_specs(nt, tq, [256], [512], t)
    return pl.pallas_call(
        functools.partial(_diff_kernel, nt=nt, tq=tq, lam_init=lam_init),
        grid=(b, nt),
        in_specs=qs + rs + [_const_spec((4, DIFF_QK)), _const_spec((1, HEAD_DIM))],
        out_specs=pl.BlockSpec((tq, 256), lambda bb, i: (bb * nt + i, 0)),
        out_shape=jax.ShapeDtypeStruct((b * t, 256), F32),
        scratch_shapes=[pltpu.VMEM((nt, tq, 256), BF16), pltpu.VMEM((nt, tq, 256), BF16)],
        compiler_params=_cparams(("arbitrary", "arbitrary")),
        name="diff_prompt",
    )(dq, rows, lam4, gain)


def _topk_mask(score, k, valid, n=None):
    n = score.shape[1] if n is None else n
    lane = lax.broadcasted_iota(jnp.int32, score.shape, 1)
    rank = jnp.zeros(score.shape, F32)
    for j in range(n):
        col = score[:, j:j + 1]
        ahead = (col > score) | ((col == score) & (lane > j))
        rank = rank + jnp.where(ahead, 1.0, 0.0)
    return jnp.where((rank < k) & valid, 1.0, 0.0)


def _lane_col(x, j):
    lane = lax.broadcasted_iota(jnp.int32, x.shape, 1)
    return jnp.sum(jnp.where(lane == j, x, 0.0), axis=1, keepdims=True)


def _moba_kernel(q_ref, rows_ref, o_ref, kb, vb, km, *, nt, tq):
    i = pl.program_id(1)

    @pl.when(i == 0)
    def _():
        for t in range(nt):
            sl = slice(t * tq, (t + 1) * tq)
            k = rows_ref[sl, 0:256]
            kb[t] = k.astype(BF16)
            vb[t] = rows_ref[sl, 256:512].astype(BF16)
            km[t:t + 1, :] = jnp.sum(k, axis=0, keepdims=True) * (1.0 / MOBA_BLOCK)

    mask = _causal(tq)
    scale = HEAD_DIM ** -0.5
    for h in range(N_HEADS_GROUP):
        hs = slice(h * HEAD_DIM, (h + 1) * HEAD_DIM)
        qf = q_ref[:, hs]
        q = qf.astype(BF16)
        gate = _dot_nt_hi(qf, km[:, hs])
        lane = lax.broadcasted_iota(jnp.int32, gate.shape, 1)
        past = lane < i
        sel = _topk_mask(jnp.where(past, gate, -jnp.inf), MOBA_TOPK, past)

        def tile(j, st, masked):
            s = _dot_nt(q, kb[j, :, hs]) * scale
            if masked:
                s = jnp.where(mask, s, NEG)
            else:
                s = jnp.where(_lane_col(sel, j) > 0.5, s, NEG)
            return _online(s, vb[j, :, hs], *st)

        st = tile(i, _init_state(tq, HEAD_DIM), True)
        m, l, acc = lax.fori_loop(0, i, lambda j, st: tile(j, st, False), st)
        o_ref[:, hs] = acc / l


def _moba_prompt(mq, rows, b, t):
    tq = MOBA_BLOCK
    assert t % tq == 0
    nt = t // tq
    qs, rs = _seq_specs(nt, tq, [256], [512], t)
    return pl.pallas_call(
        functools.partial(_moba_kernel, nt=nt, tq=tq),
        grid=(b, nt),
        in_specs=qs + rs,
        out_specs=pl.BlockSpec((tq, 256), lambda bb, i: (bb * nt + i, 0)),
        out_shape=jax.ShapeDtypeStruct((b * t, 256), F32),
        scratch_shapes=[pltpu.VMEM((nt, tq, 256), BF16), pltpu.VMEM((nt, tq, 256), BF16),
                        pltpu.VMEM((nt, 256), F32)],
        compiler_params=_cparams(("arbitrary", "arbitrary")),
        name="moba_prompt",
    )(mq, rows)


def _cmp_kernel(ck_ref, cv_ref, pos_ref, w1_ref, b1_ref, w2_ref, b2_ref, o_ref):
    half = NSA_CMP_STRIDE * HEAD_DIM
    for c, src in enumerate((ck_ref, cv_ref)):
        x = src[0]
        w1 = w1_ref[c]
        a = _dot_hi(x, w1[0:half, :])
        b = _dot_hi(x, w1[half:2 * half, :])
        const = _dot_hi(pos_ref[c], w1) + b1_ref[c]
        nch = x.shape[0]
        pre = a + pltpu.roll(b, nch - 1, 0) + const
        o_ref[0, :, c * HEAD_DIM:(c + 1) * HEAD_DIM] = _dot_hi(_gelu(pre), w2_ref[c]) + b2_ref[c]


def _nsa_compress(chunks_k, chunks_v, posflat, w1, b1, w2, b2):
    b, nch, w = chunks_k.shape
    seq = pl.BlockSpec((1, nch, w), lambda i: (i, 0, 0))
    return pl.pallas_call(
        _cmp_kernel,
        grid=(b,),
        in_specs=[seq, seq, _const_spec(posflat.shape), _const_spec(w1.shape), _const_spec(b1.shape),
                  _const_spec(w2.shape), _const_spec(b2.shape)],
        out_specs=pl.BlockSpec((1, nch, 2 * HEAD_DIM), lambda i: (i, 0, 0)),
        out_shape=jax.ShapeDtypeStruct((b, nch, 2 * HEAD_DIM), F32),
        compiler_params=_cparams(("arbitrary",)),
        name="nsa_compress",
    )(chunks_k, chunks_v, posflat, w1, b1, w2, b2)


def _imp_matrix(nc_pad, ns_pad):
    i = lax.broadcasted_iota(jnp.int32, (nc_pad, ns_pad), 0)
    j = lax.broadcasted_iota(jnp.int32, (nc_pad, ns_pad), 1)
    r = NSA_SLC_BLOCK // NSA_CMP_STRIDE
    return jnp.where((i // r == j) | ((i % r == r - 1) & (i // r + 1 == j)), 1.0, 0.0)


def _nsa_kernel(q_ref, qr_ref, g_ref, rows_ref, win_ref, ckv_ref, o_ref, sk, sv, wk, wv, *, nt, tq):
    i = pl.program_id(1)
    ns = nt * tq // NSA_SLC_BLOCK
    per = tq // NSA_SLC_BLOCK

    @pl.when(i == 0)
    def _():
        for t in range(nt):
            sl = slice(t * tq, (t + 1) * tq)
            sk[t] = rows_ref[sl, 128:192].astype(BF16)
            sv[t] = rows_ref[sl, 192:256].astype(BF16)
            wk[t] = win_ref[sl, 0:64].astype(BF16)
            wv[t] = win_ref[sl, 64:128].astype(BF16)

    scale = HEAD_DIM ** -0.5
    ckv = ckv_ref[0]
    ck, cv = ckv[:, 0:HEAD_DIM], ckv[:, HEAD_DIM:2 * HEAD_DIM]
    nc_pad = ck.shape[0]
    pos = i * tq + lax.broadcasted_iota(jnp.int32, (tq, 1), 0)

    n_id = lax.broadcasted_iota(jnp.int32, (tq, nc_pad), 1)
    m_c = (n_id * NSA_CMP_STRIDE + (NSA_CMP_LEN - 1) <= pos) & (n_id < nc_pad - 1)
    o_c = []
    psum = jnp.zeros((tq, nc_pad), F32)
    for h in range(N_HEADS_GROUP):
        hs = slice(h * HEAD_DIM, (h + 1) * HEAD_DIM)
        s = jnp.where(m_c, _dot_nt_hi(q_ref[:, hs], ck) * scale, NEG)
        mx = jnp.max(s, axis=1, keepdims=True)
        p = jnp.where(m_c, jnp.exp(s - mx), 0.0)
        den = jnp.sum(p, axis=1, keepdims=True)
        p = p / jnp.where(den > 0, den, 1.0)
        psum = psum + p
        o_c.append(_dot(p.astype(BF16), cv.astype(BF16)))
    imp = _dot_hi(psum, _imp_matrix(nc_pad, ns))
    jb = lax.broadcasted_iota(jnp.int32, (tq, ns), 1)
    cur = pos // NSA_SLC_BLOCK
    forced = (jb == 0) | (jb == cur) | (jb == cur - 1)
    valid = jb <= cur
    score = jnp.where(valid, jnp.where(forced, jnp.inf, imp), -jnp.inf)
    sel = _topk_mask(score, NSA_SLC_N, valid).astype(BF16)

    eb = lax.broadcasted_iota(jnp.int32, (ns, tq), 0)
    ec = lax.broadcasted_iota(jnp.int32, (ns, tq), 1) // NSA_SLC_BLOCK
    r_id = lax.broadcasted_iota(jnp.int32, (tq, tq), 0)
    c_id = lax.broadcasted_iota(jnp.int32, (tq, tq), 1)
    causal = r_id >= c_id

    def sel_mask(j):
        expand = jnp.where(eb == ec + j * per, 1.0, 0.0).astype(BF16)
        return _dot(sel, expand) > 0.5

    nwin = NSA_WINDOW // tq
    for h in range(N_HEADS_GROUP):
        hs = slice(h * HEAD_DIM, (h + 1) * HEAD_DIM)
        q = qr_ref[:, hs].astype(BF16)

        def s_tile(j, st, masked):
            s = _dot_nt(q, sk[j]) * scale
            ok = sel_mask(j)
            if masked:
                ok = ok & causal
            return _online(jnp.where(ok, s, NEG), sv[j], *st)

        st = s_tile(i, _init_state(tq, HEAD_DIM), True)
        m, l, acc = lax.fori_loop(0, i, lambda j, st: s_tile(j, st, False), st)
        o_s = acc / l

        def w_tile(j, st):
            s = _dot_nt(q, wk[j]) * scale
            dist = (r_id - c_id) + (i - j) * tq
            ok = (dist >= 0) & (dist < NSA_WINDOW)
            return _online(jnp.where(ok, s, NEG), wv[j], *st)

        st = w_tile(i, _init_state(tq, HEAD_DIM))
        m, l, acc = lax.fori_loop(jnp.maximum(i - nwin, 0), i, w_tile, st)
        o_w = acc / l
        g0 = 4 + 3 * h
        g = g_ref[...]
        o_ref[:, hs] = g[:, g0:g0 + 1] * o_c[h] + g[:, g0 + 1:g0 + 2] * o_s + g[:, g0 + 2:g0 + 3] * o_w


def _nsa_prompt(nq, nqr, gates, rows, win, ckv, b, t):
    tq = min(256, t)
    nt = t // tq
    qs, rs = _seq_specs(nt, tq, [256, 256, 128], [256, 128], t)
    nch = ckv.shape[1]
    return pl.pallas_call(
        functools.partial(_nsa_kernel, nt=nt, tq=tq),
        grid=(b, nt),
        in_specs=qs + rs + [pl.BlockSpec((1, nch, 2 * HEAD_DIM), lambda bb, i: (bb, 0, 0))],
        out_specs=pl.BlockSpec((tq, 256), lambda bb, i: (bb * nt + i, 0)),
        out_shape=jax.ShapeDtypeStruct((b * t, 256), F32),
        scratch_shapes=[pltpu.VMEM((nt, tq, HEAD_DIM), BF16) for _ in range(4)],
        compiler_params=_cparams(("arbitrary", "arbitrary")),
        name="nsa_prompt",
    )(nq, nqr, gates, rows, win, ckv)


def _outproj_kernel(oa_ref, ob_ref, oc_ref, od_ref, x_ref, whi_ref, wlo_ref, g_ref, b_ref, h_ref, *, alpha):
    acc = alpha * x_ref[...]
    for gi, o_ref in enumerate((oa_ref, ob_ref, oc_ref, od_ref)):
        hi, lo = _split(o_ref[...])
        rs = slice(gi * GROUP_W, (gi + 1) * GROUP_W)
        acc = acc + _mm3(hi, lo, whi_ref[rs, :], wlo_ref[rs, :])
    h_ref[...] = _layernorm(acc, g_ref[...], b_ref[...])


def _outproj(outs, x, whi, wlo, g, b, alpha):
    n = x.shape[0]
    tb = min(256, n)
    tok = lambda w: pl.BlockSpec((tb, w), lambda i: (i, 0))
    return pl.pallas_call(
        functools.partial(_outproj_kernel, alpha=alpha),
        grid=(n // tb,),
        in_specs=[tok(GROUP_W)] * 4 + [tok(D_MODEL), _const_spec(whi.shape), _const_spec(wlo.shape),
                                       _const_spec((1, D_MODEL)), _const_spec((1, D_MODEL))],
        out_specs=tok(D_MODEL),
        out_shape=jax.ShapeDtypeStruct((n, D_MODEL), F32),
        compiler_params=_cparams(("arbitrary",)),
        name="outproj_ln",
    )(*outs, x, whi, wlo, g, b)


_STAIR = [(r1, r2) for r1 in range(PEER_TOPK) for r2 in range(PEER_TOPK) if (r1 + 1) * (r2 + 1) <= PEER_TOPK]
_NCAND = -(-len(_STAIR) // 8) * 8


def _top_values(x, k, out_ref):
    cur = x
    m = None
    for r in range(k):
        m = jnp.max(cur, axis=0, keepdims=True)
        if out_ref is not None:
            out_ref[r:r + 1, :] = m
        cur = jnp.where(cur == m, -jnp.inf, cur)
    return m


def _peer_kernel(h_ref, wqhi_ref, wqlo_ref, sub_ref, u_ref, vt_ref, g_ref, b_ref, y_ref,
                 hb, s1, s2, ga, gb, thr, v1s, v2s, cand, wg, acc, *, ec, alpha):
    k = pl.program_id(1)
    nk = pl.num_programs(1)
    per = ec // PEER_NKEYS

    @pl.when(k == 0)
    def _():
        hhi, hlo = _split(h_ref[...])
        hb[...] = hhi
        qp = _mm3(hhi, hlo, wqhi_ref[...], wqlo_ref[...])
        half = HEAD_DIM
        for hd in range(PEER_HEADS):
            for c, (sdst, vdst) in enumerate(((s1, v1s), (s2, v2s))):
                c0 = (hd * 2 + c) * half
                st = _dot_nt_hi(sub_ref[c], qp[:, c0:c0 + half])
                sdst[hd] = st
                _top_values(st, PEER_TOPK, vdst)
            v1, v2 = v1s[...], v2s[...]
            cand[...] = jnp.full(cand.shape, -jnp.inf, F32)
            for idx, (r1, r2) in enumerate(_STAIR):
                cand[idx:idx + 1, :] = v1[r1:r1 + 1, :] + v2[r2:r2 + 1, :]
            cv = cand[...]
            th = _top_values(cv, PEER_TOPK, None)
            cmax = v1[0:1, :] + v2[0:1, :]
            z = jnp.sum(jnp.where(cv >= th, jnp.exp(cv - cmax), 0.0), axis=0, keepdims=True)
            thr[hd:hd + 1, :] = th
            ga[hd] = jnp.exp(s1[hd] - v1[0:1, :]) / z
            gb[hd] = jnp.exp(s2[hd] - v2[0:1, :])
        acc[...] = jnp.zeros(acc.shape, F32)

    ht = _dot_nt(u_ref[...], hb[...])
    for il in range(per):
        i1 = k * per + il
        act = _gelu(ht[il * PEER_NKEYS:(il + 1) * PEER_NKEYS, :])
        w = jnp.zeros(act.shape, F32)
        for hd in range(PEER_HEADS):
            ssum = s1[hd, pl.ds(i1, 1), :] + s2[hd]
            w = w + jnp.where(ssum >= thr[hd:hd + 1, :], gb[hd], 0.0) * ga[hd, pl.ds(i1, 1), :]
        wg[il * PEER_NKEYS:(il + 1) * PEER_NKEYS, :] = (act * w).astype(BF16)
    acc[...] += _dot(vt_ref[...], wg[...])

    @pl.when(k == nk - 1)
    def _():
        y_ref[...] = _layernorm(alpha * h_ref[...] + acc[...].T, g_ref[...], b_ref[...])


def _peer(h, wqhi, wqlo, subkeys, u_bf, vt_bf, g, b, alpha, tb=256, ec=1024):
    n = h.shape[0]
    tb = min(tb, n)
    n_exp = u_bf.shape[0]
    tshape = lambda r: pltpu.VMEM((PEER_HEADS, r, tb), F32)
    return pl.pallas_call(
        functools.partial(_peer_kernel, ec=ec, alpha=alpha),
        grid=(n // tb, n_exp // ec),
        in_specs=[pl.BlockSpec((tb, D_MODEL), lambda i, k: (i, 0)),
                  _const_spec(wqhi.shape), _const_spec(wqlo.shape), _const_spec(subkeys.shape),
                  pl.BlockSpec((ec, D_MODEL), lambda i, k: (k, 0)),
                  pl.BlockSpec((D_MODEL, ec), lambda i, k: (0, k)),
                  _const_spec((1, D_MODEL)), _const_spec((1, D_MODEL))],
        out_specs=pl.BlockSpec((tb, D_MODEL), lambda i, k: (i, 0)),
        out_shape=jax.ShapeDtypeStruct((n, D_MODEL), F32),
        scratch_shapes=[pltpu.VMEM((tb, D_MODEL), BF16),
                        tshape(PEER_NKEYS), tshape(PEER_NKEYS), tshape(PEER_NKEYS), tshape(PEER_NKEYS),
                        pltpu.VMEM((PEER_HEADS, tb), F32),
                        pltpu.VMEM((PEER_TOPK, tb), F32), pltpu.VMEM((PEER_TOPK, tb), F32),
                        pltpu.VMEM((_NCAND, tb), F32),
                        pltpu.VMEM((ec, tb), BF16), pltpu.VMEM((D_MODEL, tb), F32)],
        compiler_params=_cparams(("arbitrary", "arbitrary")),
        name="peer_ln",
    )(h, wqhi, wqlo, subkeys, u_bf, vt_bf, g, b)


def _relayout_w_in(w):
    ff = w[:, 768:772]
    ng = w[:, 2948:2960]
    body = jnp.concatenate([w[:, 0:768], w[:, 772:2948]], axis=1)
    pad = jnp.zeros((w.shape[0], D_IN_PAD - _C_SMALL - 16), w.dtype)
    return jnp.concatenate([body, ff, ng, pad], axis=1)


def _hilo(w):
    hi = w.astype(BF16)
    return hi, (w - hi.astype(F32)).astype(BF16)


def _prep_layer(w_in, fox_bf, diff_lam, diff_gain, cmp_pos, cmp_w1, cmp_b1, cmp_w2, cmp_b2,
                w_out, ln1_g, ln1_b, peer_wq, peer_subkeys, peer_u, peer_v, ln2_g, ln2_b):
    row = lambda a: a.reshape(1, -1)
    return dict(
        w_in=_hilo(_relayout_w_in(w_in)),
        bf=jnp.pad(fox_bf, (0, LANES - fox_bf.shape[0])).reshape(1, LANES),
        lam=diff_lam, gain=row(diff_gain),
        cmp_pos=cmp_pos.reshape(2, 1, -1), cmp_w1=cmp_w1, cmp_b1=cmp_b1.reshape(2, 1, -1),
        cmp_w2=cmp_w2, cmp_b2=cmp_b2.reshape(2, 1, -1),
        w_out=_hilo(w_out), ln1=(row(ln1_g), row(ln1_b)),
        wq=_hilo(peer_wq), sub=peer_subkeys,
        u=peer_u.astype(BF16), vt=peer_v.T.astype(BF16), ln2=(row(ln2_g), row(ln2_b)))


def _ffn(h, lw, alpha):
    n = h.shape[0]
    npad = -(-n // LANES) * LANES
    hp = jnp.pad(h, ((0, npad - n), (0, 0))) if npad != n else h
    y = _peer(hp, *lw["wq"], lw["sub"], lw["u"], lw["vt"], *lw["ln2"], alpha)
    return y[:n]


def _prompt_layer(x, lw, b, t, lam_init, alpha):
    pos = jnp.arange(t, dtype=jnp.int32)
    (fox, diffr, mobar, nsar, win, lf, fq, dq, mq, nq, nqr, gates) = _inproj(x, *lw["w_in"], lw["bf"], pos, t)
    o_a = _fox_prompt(fq, fox, lf, b, t)
    o_b = _diff_prompt(dq, diffr, lw["lam"], lw["gain"], b, t, lam_init)
    o_c = _moba_prompt(mq, mobar, b, t)
    nch = t // NSA_CMP_STRIDE
    chunks = lambda c: nsar[:, c * HEAD_DIM:(c + 1) * HEAD_DIM].reshape(b, nch, NSA_CMP_STRIDE * HEAD_DIM)
    ckv = _nsa_compress(chunks(0), chunks(1), lw["cmp_pos"], lw["cmp_w1"], lw["cmp_b1"], lw["cmp_w2"], lw["cmp_b2"])
    o_d = _nsa_prompt(nq, nqr, gates, nsar, win, ckv, b, t)
    h = _outproj((o_a, o_b, o_c, o_d), x, *lw["w_out"], *lw["ln1"], alpha)
    y = _ffn(h, lw, alpha)
    return y, (fox, diffr, mobar, nsar, win), (o_a, o_b, o_c, o_d, h)


PAGES_PER_STEP = 8
ROWS = 8


def _page_specs(width, layer, pp, rows=PAGE):
    def spec(r):
        return pl.BlockSpec((1, 1, rows, width), lambda b, p, pt: (layer, pt[b, p * pp + r], 0, 0))
    return [spec(r) for r in range(pp)]


def _seq3(shape):
    return pl.BlockSpec((1,) + shape, lambda b, p, pt: (b, 0, 0))


def _online_t(s, v, m, l, acc):
    m_new = jnp.maximum(m, jnp.max(s, axis=1, keepdims=True))
    a = jnp.exp(m - m_new)
    p = jnp.exp(s - m_new)
    return m_new, a * l + jnp.sum(p, axis=1, keepdims=True), a * acc + _dot(p.astype(BF16), v)


def _add_new(s_new, v_new, m, l, acc):
    m_new = jnp.maximum(m, s_new)
    a = jnp.exp(m - m_new)
    p = jnp.exp(s_new - m_new)
    return (a * acc + p * v_new) / (a * l + p)


def _head_diag(o8, rows_per_head):
    r = lax.broadcasted_iota(jnp.int32, o8.shape, 0)
    lane = lax.broadcasted_iota(jnp.int32, o8.shape, 1)
    return jnp.where(lane // HEAD_DIM == r // rows_per_head, o8, 0.0)


def _fox_dec_kernel(pt_ref, q_ref, new_ref, lfn_ref, *rest, pp):
    pages, lfts = rest[:pp], rest[pp:2 * pp]
    o_ref, m_s, l_s, acc_s, car_s = rest[2 * pp:]
    p = pl.program_id(1)

    @pl.when(p == 0)
    def _():
        m_s[...] = jnp.full(m_s.shape, NEG, F32)
        l_s[...] = jnp.zeros(l_s.shape, F32)
        acc_s[...] = jnp.zeros(acc_s.shape, F32)
        car_s[...] = jnp.zeros(car_s.shape, F32)

    q = q_ref[0]
    qb = q.astype(BF16)
    ri = lax.broadcasted_iota(jnp.int32, (PAGE, PAGE), 0)
    ci = lax.broadcasted_iota(jnp.int32, (PAGE, PAGE), 1)
    upper = (ri <= ci).astype(F32)
    st = (m_s[...], l_s[...], acc_s[...])
    carry = car_s[...]
    for r in range(pp):
        f = _dot_hi(lfts[r][0, 0], upper) + carry
        carry = f[:, PAGE - 1:PAGE]
        s = _dot_nt(qb, pages[r][0, 0, :, 0:256].astype(BF16)) - f
        st = _online_t(s, pages[r][0, 0, :, 256:512].astype(BF16), *st)
    m_s[...], l_s[...], acc_s[...] = st
    car_s[...] = carry

    @pl.when(p == pl.num_programs(1) - 1)
    def _():
        new = new_ref[0]
        s_new = jnp.sum(q * new[:, 0:256], axis=1, keepdims=True) - (carry + lfn_ref[0])
        o8 = _add_new(s_new, new[:, 256:512], *st)
        o_ref[0] = jnp.sum(_head_diag(o8, 1), axis=0, keepdims=True)


def _decode_call(kern, name, pt, seq_ops, page_ops, out_shapes, scratch, n_pages, pp):
    b = pt.shape[0]
    in_specs = [_seq3(a.shape[1:]) for a in seq_ops]
    args = list(seq_ops)
    for arr, layer in page_ops:
        in_specs += _page_specs(arr.shape[-1], layer, pp, arr.shape[-2])
        args += [arr] * pp
    return pl.pallas_call(
        kern,
        grid_spec=pltpu.PrefetchScalarGridSpec(
            num_scalar_prefetch=1, grid=(b, n_pages // pp), in_specs=in_specs,
            out_specs=[_seq3(s[1:]) for s in out_shapes], scratch_shapes=scratch),
        out_shape=[jax.ShapeDtypeStruct(s, F32) for s in out_shapes],
        compiler_params=_cparams(("arbitrary", "arbitrary")),
        name=name,
    )(pt, *args)


def _softmax_state(dv):
    return [pltpu.VMEM((ROWS, 1), F32), pltpu.VMEM((ROWS, 1), F32), pltpu.VMEM((ROWS, dv), F32)]


def _query_rows(q, rows_per_head, scale):
    w = HEAD_DIM // rows_per_head
    n_maps = GROUP_W // w
    lane_map = jnp.arange(GROUP_W) // w
    sel = (lane_map[None, :] == jnp.arange(ROWS)[:, None]) & (jnp.arange(ROWS)[:, None] < n_maps)
    return jnp.where(sel[None], q[:, None, :] * scale, 0.0)


def _fox_decode(pt, fq, new_row, lf_new, cache, lft, layer, n_pages, pp):
    b = pt.shape[0]
    q8 = _query_rows(fq, 1, HEAD_DIM ** -0.5)
    lfn = jnp.pad(lf_new[:, :N_HEADS_GROUP], ((0, 0), (0, ROWS - N_HEADS_GROUP))).reshape(b, ROWS, 1)
    (o,) = _decode_call(
        functools.partial(_fox_dec_kernel, pp=pp), "fox_decode", pt,
        [q8, new_row.reshape(b, 1, -1), lfn], [(cache, layer), (lft, layer)],
        [(b, 1, GROUP_W)], _softmax_state(GROUP_W) + [pltpu.VMEM((ROWS, 1), F32)], n_pages, pp)
    return o.reshape(b, GROUP_W)


def _diff_dec_kernel(pt_ref, q_ref, new_ref, lam_ref, gain_ref, *rest, pp, lam_init):
    pages = rest[:pp]
    o_ref, m_s, l_s, acc_s = rest[pp:]
    p = pl.program_id(1)

    @pl.when(p == 0)
    def _():
        m_s[...] = jnp.full(m_s.shape, NEG, F32)
        l_s[...] = jnp.zeros(l_s.shape, F32)
        acc_s[...] = jnp.zeros(acc_s.shape, F32)

    q = q_ref[0]
    qb = q.astype(BF16)
    scale = DIFF_QK ** -0.5
    st = (m_s[...], l_s[...], acc_s[...])
    for r in range(pp):
        s = _dot_nt(qb, pages[r][0, 0, :, 0:256].astype(BF16)) * scale
        st = _online_t(s, pages[r][0, 0, :, 256:512].astype(BF16), *st)
    m_s[...], l_s[...], acc_s[...] = st

    @pl.when(p == pl.num_programs(1) - 1)
    def _():
        new = new_ref[0]
        s_new = jnp.sum(q * new[:, 0:256], axis=1, keepdims=True) * scale
        o8 = _head_diag(_add_new(s_new, new[:, 256:512], *st), 2)
        r = lax.broadcasted_iota(jnp.int32, o8.shape, 0)
        o0 = jnp.sum(jnp.where(r % 2 == 0, o8, 0.0), axis=0, keepdims=True)
        o1 = jnp.sum(jnp.where(r % 2 == 1, o8, 0.0), axis=0, keepdims=True)
        o = o0 - _diff_lambda(lam_ref, lam_init) * o1
        gi = lax.broadcasted_iota(jnp.int32, (GROUP_W, GROUP_W), 0) // HEAD_DIM
        gj = lax.broadcasted_iota(jnp.int32, (GROUP_W, GROUP_W), 1) // HEAD_DIM
        ms = _dot_hi(jnp.broadcast_to(o * o, (ROWS, GROUP_W)), jnp.where(gi == gj, 1.0 / HEAD_DIM, 0.0))[0:1]
        o_ref[0] = o * lax.rsqrt(ms + LN_EPS) * gain_ref[...] * (1.0 - lam_init)


def _diff_decode(pt, dq, new_row, lam4, gain, cache, layer, n_pages, pp, lam_init):
    b = pt.shape[0]
    q8 = _query_rows(dq, 2, 1.0)
    gain4 = jnp.tile(gain, (1, N_HEADS_GROUP))
    kern = functools.partial(_diff_dec_kernel, pp=pp, lam_init=lam_init)
    in_specs = [_seq3(q8.shape[1:]), _seq3((1, new_row.shape[-1])),
                pl.BlockSpec(lam4.shape, lambda b_, p, pt_: (0, 0)),
                pl.BlockSpec(gain4.shape, lambda b_, p, pt_: (0, 0))] + _page_specs(cache.shape[-1], layer, pp)
    o = pl.pallas_call(
        kern,
        grid_spec=pltpu.PrefetchScalarGridSpec(
            num_scalar_prefetch=1, grid=(b, n_pages // pp), in_specs=in_specs,
            out_specs=_seq3((1, GROUP_W)), scratch_shapes=_softmax_state(GROUP_W)),
        out_shape=jax.ShapeDtypeStruct((b, 1, GROUP_W), F32),
        compiler_params=_cparams(("arbitrary", "arbitrary")),
        name="diff_decode",
    )(pt, q8, new_row.reshape(b, 1, -1), lam4, gain4, *([cache] * pp))
    return o.reshape(b, GROUP_W)


def _moba_dec_kernel(pt_ref, q_ref, new_ref, *rest, pp, n_blocks):
    pages = rest[:pp]
    o_ref, gate_s, mb_s, lb_s, acc_s = rest[pp:]
    p = pl.program_id(1)
    ppb = MOBA_BLOCK // PAGE
    q = q_ref[0]
    qb = q.astype(BF16)
    lane = lax.broadcasted_iota(jnp.int32, (ROWS, LANES), 1)

    @pl.when(p == 0)
    def _():
        gate_s[...] = jnp.zeros(gate_s.shape, F32)
        mb_s[...] = jnp.full(mb_s.shape, NEG, F32)
        lb_s[...] = jnp.zeros(lb_s.shape, F32)

    for blk in range(pp // ppb):
        bi = p * (pp // ppb) + blk
        ksum = jnp.zeros((1, GROUP_W), F32)
        st = (jnp.full((ROWS, 1), NEG, F32), jnp.zeros((ROWS, 1), F32), jnp.zeros((ROWS, GROUP_W), F32))
        for r in range(blk * ppb, (blk + 1) * ppb):
            k = pages[r][0, 0, :, 0:256]
            ksum = ksum + jnp.sum(k, axis=0, keepdims=True)
            st = _online_t(_dot_nt(qb, k.astype(BF16)), pages[r][0, 0, :, 256:512].astype(BF16), *st)
        gate = jnp.sum(q * ksum, axis=1, keepdims=True) * (HEAD_DIM ** 0.5 / MOBA_BLOCK)
        gate_s[...] = jnp.where(lane == bi, gate, gate_s[...])
        mb_s[...] = jnp.where(lane == bi, st[0], mb_s[...])
        lb_s[...] = jnp.where(lane == bi, st[1], lb_s[...])
        acc_s[bi] = st[2]

    @pl.when(p == pl.num_programs(1) - 1)
    def _():
        past = lane < n_blocks
        sel = _topk_mask(jnp.where(past, gate_s[...], -jnp.inf), MOBA_TOPK, past)
        new = new_ref[0]
        s_new = jnp.sum(q * new[:, 0:256], axis=1, keepdims=True)
        mb = jnp.where(sel > 0.5, mb_s[...], NEG)
        m = jnp.maximum(jnp.max(mb, axis=1, keepdims=True), s_new)
        w = sel * jnp.exp(mb - m)
        p_new = jnp.exp(s_new - m)
        den = jnp.sum(w * lb_s[...], axis=1, keepdims=True) + p_new

        def body(bi, acc):
            return acc + _lane_col(w, bi) * acc_s[bi]

        acc = lax.fori_loop(0, n_blocks, body, p_new * new[:, 256:512])
        o_ref[0] = jnp.sum(_head_diag(acc / den, 1), axis=0, keepdims=True)


def _moba_decode(pt, mq, new_row, cache, layer, n_pages, pp):
    b = pt.shape[0]
    n_blocks = n_pages * PAGE // MOBA_BLOCK
    assert n_blocks <= LANES
    q8 = _query_rows(mq, 1, HEAD_DIM ** -0.5)
    (o,) = _decode_call(
        functools.partial(_moba_dec_kernel, pp=pp, n_blocks=n_blocks), "moba_decode", pt,
        [q8, new_row.reshape(b, 1, -1)], [(cache, layer)], [(b, 1, GROUP_W)],
        [pltpu.VMEM((ROWS, LANES), F32)] * 3 + [pltpu.VMEM((n_blocks, ROWS, GROUP_W), F32)], n_pages, pp)
    return o.reshape(b, GROUP_W)


def _nsa_dec_kernel(pt_ref, q_ref, qr_ref, g_ref, new_ref, wnew_ref, win_ref, pos_ref, w1_ref, b1_ref,
                    w2_ref, b2_ref, *rest, pp, n_pages):
    pages = rest[:pp]
    o_ref, nwin_ref, seqc, seqs, ckv = rest[pp:]
    p = pl.program_id(1)
    past = n_pages * PAGE
    for r in range(pp):
        rows = pl.ds((p * pp + r) * PAGE, PAGE)
        seqc[rows, :] = pages[r][0, 0, :, 0:2 * HEAD_DIM]
        seqs[rows, :] = pages[r][0, 0, :, 2 * HEAD_DIM:4 * HEAD_DIM]

    @pl.when(p == pl.num_programs(1) - 1)
    def _():
        scale = HEAD_DIM ** -0.5
        nch = past // NSA_CMP_STRIDE
        half = NSA_CMP_STRIDE * HEAD_DIM
        for c in range(2):
            cs = slice(c * HEAD_DIM, (c + 1) * HEAD_DIM)
            w1 = w1_ref[c]
            a = jnp.zeros((nch, NSA_CMP_HIDDEN), F32)
            bb = jnp.zeros((nch, NSA_CMP_HIDDEN), F32)
            for r in range(NSA_CMP_STRIDE):
                x = seqc[pl.ds(r, nch, stride=NSA_CMP_STRIDE), :][:, cs]
                a = a + _dot_hi(x, w1[r * HEAD_DIM:(r + 1) * HEAD_DIM, :])
                bb = bb + _dot_hi(x, w1[half + r * HEAD_DIM:half + (r + 1) * HEAD_DIM, :])
            const = _dot_hi(pos_ref[c], w1) + b1_ref[c]
            pre = a + pltpu.roll(bb, nch - 1, 0) + const
            ckv[:, cs] = _dot_hi(_gelu(pre), w2_ref[c]) + b2_ref[c]
        q = q_ref[0]
        qr = qr_ref[0]
        qrb = qr.astype(BF16)
        rowv = lax.broadcasted_iota(jnp.int32, (ROWS, 1), 0) < N_HEADS_GROUP
        n_id = lax.broadcasted_iota(jnp.int32, (ROWS, nch), 1)
        m_c = (n_id * NSA_CMP_STRIDE + (NSA_CMP_LEN - 1) <= past) & (n_id < nch - 1)
        s = jnp.where(m_c, _dot_nt_hi(q, ckv[:, 0:HEAD_DIM]) * scale, NEG)
        pc = jnp.where(m_c, jnp.exp(s - jnp.max(s, axis=1, keepdims=True)), 0.0)
        den = jnp.sum(pc, axis=1, keepdims=True)
        pc = pc / jnp.where(den > 0, den, 1.0)
        o_c = _dot(pc.astype(BF16), ckv[:, HEAD_DIM:2 * HEAD_DIM].astype(BF16))
        psum = jnp.sum(jnp.where(rowv, pc, 0.0), axis=0, keepdims=True)
        ns = past // NSA_SLC_BLOCK + 1
        ns_pad = -(-ns // LANES) * LANES
        imp = _dot_hi(jnp.broadcast_to(psum, (ROWS, nch)), _imp_matrix(nch, ns_pad))
        jb = lax.broadcasted_iota(jnp.int32, (ROWS, ns_pad), 1)
        cur = past // NSA_SLC_BLOCK
        forced = (jb == 0) | (jb == cur) | (jb == cur - 1)
        valid = jb <= cur
        score = jnp.where(valid, jnp.where(forced, jnp.inf, imp), -jnp.inf)
        sel = _topk_mask(score, NSA_SLC_N, valid, n=ns).astype(BF16)
        ch = 2048
        per = ch // NSA_SLC_BLOCK
        eb = lax.broadcasted_iota(jnp.int32, (per, ch), 0)
        ec = lax.broadcasted_iota(jnp.int32, (per, ch), 1) // NSA_SLC_BLOCK
        expand = jnp.where(eb == ec, 1.0, 0.0).astype(BF16)
        st = (jnp.full((ROWS, 1), NEG, F32), jnp.zeros((ROWS, 1), F32), jnp.zeros((ROWS, HEAD_DIM), F32))
        for c in range(past // ch):
            ks = seqs[c * ch:(c + 1) * ch, 0:HEAD_DIM].astype(BF16)
            vs = seqs[c * ch:(c + 1) * ch, HEAD_DIM:2 * HEAD_DIM].astype(BF16)
            ok = _dot(sel[:, c * per:(c + 1) * per], expand) > 0.5
            st = _online_t(jnp.where(ok, _dot_nt(qrb, ks) * scale, NEG), vs, *st)
        new = new_ref[0]
        s_new = jnp.sum(qr * new[:, 2 * HEAD_DIM:3 * HEAD_DIM], axis=1, keepdims=True) * scale
        o_s = _add_new(s_new, new[:, 3 * HEAD_DIM:4 * HEAD_DIM], *st)
        wrow = wnew_ref[0]
        state = win_ref[0]
        wb = state.shape[0]
        kpos = lax.broadcasted_iota(jnp.int32, (ROWS, wb), 1)
        sw = jnp.where(kpos >= wb + 1 - NSA_WINDOW, _dot_nt(qrb, state[:, 0:HEAD_DIM].astype(BF16)) * scale, NEG)
        st = _online_t(sw, state[:, HEAD_DIM:2 * HEAD_DIM].astype(BF16),
                       jnp.full((ROWS, 1), NEG, F32), jnp.zeros((ROWS, 1), F32), jnp.zeros((ROWS, HEAD_DIM), F32))
        s_new = jnp.sum(qr * wrow[:, 0:HEAD_DIM], axis=1, keepdims=True) * scale
        o_w = _add_new(s_new, wrow[:, HEAD_DIM:2 * HEAD_DIM], *st)
        g = g_ref[0]
        o_ref[0] = g[:, 0:1] * o_c + g[:, 1:2] * o_s + g[:, 2:3] * o_w
        nwin_ref[0, 0:wb - 1, :] = win_ref[0, 1:wb, :]
        nwin_ref[0, wb - 1:wb, :] = wrow


def _nsa_decode(pt, nq, nqr, gates, new_row, win_new, win_state, lw, cache, layer, n_pages, pp):
    b = pt.shape[0]
    heads = lambda a: jnp.pad(a.reshape(b, N_HEADS_GROUP, HEAD_DIM), ((0, 0), (0, ROWS - N_HEADS_GROUP), (0, 0)))
    g = gates[:, 4:16].reshape(b, N_HEADS_GROUP, 3)
    g = jnp.pad(g, ((0, 0), (0, ROWS - N_HEADS_GROUP), (0, LANES - 3)))
    wb = win_state.shape[1]
    past = n_pages * PAGE
    assert past % 2048 == 0 and wb == NSA_WINDOW
    consts = [lw["cmp_pos"], lw["cmp_w1"], lw["cmp_b1"], lw["cmp_w2"], lw["cmp_b2"]]
    seq_ops = [heads(nq), heads(nqr), g, new_row.reshape(b, 1, -1), win_new.reshape(b, 1, -1), win_state]
    in_specs = ([_seq3(a.shape[1:]) for a in seq_ops]
                + [pl.BlockSpec(c.shape, lambda b_, p, pt_, nd=c.ndim: (0,) * nd) for c in consts]
                + _page_specs(cache.shape[-1], layer, pp))
    o, nwin = pl.pallas_call(
        functools.partial(_nsa_dec_kernel, pp=pp, n_pages=n_pages),
        grid_spec=pltpu.PrefetchScalarGridSpec(
            num_scalar_prefetch=1, grid=(b, n_pages // pp), in_specs=in_specs,
            out_specs=[_seq3((ROWS, HEAD_DIM)), _seq3((wb, 2 * HEAD_DIM))],
            scratch_shapes=[pltpu.VMEM((past, 2 * HEAD_DIM), F32), pltpu.VMEM((past, 2 * HEAD_DIM), F32),
                            pltpu.VMEM((past // NSA_CMP_STRIDE, 2 * HEAD_DIM), F32)]),
        out_shape=[jax.ShapeDtypeStruct((b, ROWS, HEAD_DIM), F32), jax.ShapeDtypeStruct((b, wb, 2 * HEAD_DIM), F32)],
        compiler_params=_cparams(("arbitrary", "arbitrary")),
        name="nsa_decode",
    )(pt, *seq_ops, *consts, *([cache] * pp))
    return o[:, :N_HEADS_GROUP].reshape(b, GROUP_W), nwin


def _sample_layer(x, lw, caches, lft, win_state, pt, layer, lam_init, alpha, pp=PAGES_PER_STEP):
    b = x.shape[0]
    n_pages = pt.shape[1]
    pos = jnp.full((1,), n_pages * PAGE, jnp.int32)
    (fox, diffr, mobar, nsar, win, lf, fq, dq, mq, nq, nqr, gates) = _inproj(x, *lw["w_in"], lw["bf"], pos, 1)
    cache_fox, cache_diff, cache_moba, cache_nsa = caches
    o_a = _fox_decode(pt, fq, fox, lf, cache_fox, lft, layer, n_pages, pp)
    o_b = _diff_decode(pt, dq, diffr, lw["lam"], lw["gain"], cache_diff, layer, n_pages, pp, lam_init)
    o_c = _moba_decode(pt, mq, mobar, cache_moba, layer, n_pages, pp)
    o_d, new_win = _nsa_decode(pt, nq, nqr, gates, nsar, win, win_state, lw, cache_nsa, layer, n_pages, pp)
    h = _outproj((o_a, o_b, o_c, o_d), x, *lw["w_out"], *lw["ln1"], alpha)
    y = _ffn(h, lw, alpha)
    return y, (fox, diffr, mobar, nsar, new_win), (o_a, o_b, o_c, o_d, h)


def _forget_rows(cache_fox):
    lf = jnp.swapaxes(cache_fox[..., 2 * GROUP_W:], -1, -2)
    return jnp.pad(lf, ((0, 0), (0, 0), (0, ROWS - N_HEADS_GROUP), (0, 0)))


def kernel(x_prompt, x_sample, cache_fox, cache_diff, cache_moba, cache_nsa, state_nsa_win, page_table,
           w_in, fox_bf, diff_lam, diff_gain, nsa_cmp_pos, nsa_cmp_w1, nsa_cmp_b1, nsa_cmp_w2, nsa_cmp_b2,
           w_out, ln1_g, ln1_b, peer_wq, peer_subkeys, peer_u, peer_v, ln2_g, ln2_b):
    b, t, d = x_prompt.shape
    bs, ts, _ = x_sample.shape
    assert ts == 1
    depth = w_in.shape[0]
    alpha = (2.0 * depth) ** 0.25
    yp = x_prompt.reshape(b * t, d)
    ys = x_sample.reshape(bs * ts, d)
    lft = _forget_rows(cache_fox)
    caches = (cache_fox, cache_diff, cache_moba, cache_nsa)
    new_p = [[] for _ in range(5)]
    new_s = [[] for _ in range(5)]
    for l in range(depth):
        lam_init = 0.8 - 0.6 * math.exp(-0.3 * l)
        lw = _prep_layer(w_in[l], fox_bf[l], diff_lam[l], diff_gain[l], nsa_cmp_pos[l], nsa_cmp_w1[l],
                         nsa_cmp_b1[l], nsa_cmp_w2[l], nsa_cmp_b2[l], w_out[l], ln1_g[l], ln1_b[l],
                         peer_wq[l], peer_subkeys[l], peer_u[l], peer_v[l], ln2_g[l], ln2_b[l])
        yp, rows, _ = _prompt_layer(yp, lw, b, t, lam_init, alpha)
        ys, srows, _ = _sample_layer(ys, lw, caches, lft, state_nsa_win[l], page_table, l, lam_init, alpha)
        for i in range(4):
            new_p[i].append(rows[i].reshape(b, t, -1))
            new_s[i].append(srows[i].reshape(bs, ts, -1))
        new_p[4].append(rows[4].reshape(b, t, -1)[:, -min(NSA_WINDOW, t):])
        new_s[4].append(srows[4])
    out = [yp.reshape(b, t, d), ys.reshape(bs, ts, d)]
    for i in range(5):
        out += [jnp.stack(new_p[i]), jnp.stack(new_s[i])]
    return tuple(out)
```
